```python
import jax, jax.numpy as jnp
from jax import lax
import numpy as np

D_MODEL = 1024
BATCH = 4
SEQ = 4096
DEPTH = 4

CHUNK = 64
N_MEM = 256
N_BRANCH = 4
GROUPS = 4
GROUP_W = D_MODEL // 8
MIX_W = GROUPS * GROUP_W
SHORT_K = 3
POOL_WINDOWS = (2, 4, 8, 16)
SGU_BLOCK = 128
CONF_K = 31
D_FF = 2816
XA_HEADS = 4
XA_HEAD_DIM = D_MODEL // XA_HEADS
EPS = 1e-6
IN_COLS = 3 * MIX_W + MIX_W + 2 * MIX_W + 2 * MIX_W
IN_SPLITS = (MIX_W, 2 * MIX_W, 3 * MIX_W, 4 * MIX_W, 6 * MIX_W)

kernel_name = "hybrid_gated_conv_pool_sgu_conformer_block"


def rmsnorm(x, g):
    xf = x.astype(jnp.float32)
    y = xf * lax.rsqrt(jnp.mean(xf * xf, axis=-1, keepdims=True) + EPS)
    return (y * g.astype(jnp.float32)).astype(x.dtype)


def layernorm(x, g, b):
    xf = x.astype(jnp.float32)
    mu = jnp.mean(xf, axis=-1, keepdims=True)
    var = jnp.mean(jnp.square(xf - mu), axis=-1, keepdims=True)
    y = (xf - mu) * lax.rsqrt(var + EPS)
    return (y * g.astype(jnp.float32) + b.astype(jnp.float32)).astype(x.dtype)


def causal_depthwise_conv(x, w, b):
    k, c = w.shape
    y = lax.conv_general_dilated(x, w[:, None, :].astype(x.dtype), window_strides=(1,),
                                 padding=[(k - 1, 0)],
                                 dimension_numbers=('NWC', 'WIO', 'NWC'),
                                 feature_group_count=c)
    return y + b


def swiglu(h, w1, w3, w2):
    return (jax.nn.silu(h @ w1) * (h @ w3)) @ w2


def short_conv_mixer(xa, ba, ca, w, b):
    return ba * causal_depthwise_conv(ca * xa, w, b)


def pool_mixer(p, w_grp, scale):
    bsz, s, _ = p.shape
    pg = p.reshape(bsz, s, GROUPS, GROUP_W)
    pos = jnp.arange(1, s + 1, dtype=jnp.float32)
    outs = []
    for gi, win in enumerate(POOL_WINDOWS):
        xg = pg[:, :, gi].astype(jnp.float32)
        cs = jnp.cumsum(xg, axis=1)
        lag = jnp.pad(cs, ((0, 0), (win, 0), (0, 0)))[:, :s]
        mean = (cs - lag) / jnp.minimum(pos, float(win))[None, :, None]
        outs.append(mean - xg)
    pooled = jnp.stack(outs, axis=2).astype(p.dtype)
    mixed = jnp.einsum('bsgc,gcd->bsgd', pooled, w_grp)
    return mixed.reshape(bsz, s, MIX_W) * scale


def sgu_mixer(gc, ln_g, ln_b, ws, bs):
    bsz, s, _ = gc.shape
    gc = jax.nn.gelu(gc)
    u, v = jnp.split(gc, 2, axis=-1)
    v = layernorm(v, ln_g, ln_b)
    cidx = jnp.arange(SGU_BLOCK) // CHUNK
    mask = cidx[None, :] <= cidx[:, None]
    wm = jnp.where(mask[None], ws, jnp.zeros_like(ws))
    vb = v.reshape(bsz, s // SGU_BLOCK, SGU_BLOCK, GROUPS, GROUP_W)
    mixed = jnp.einsum('gij,bnjgc->bnigc', wm, vb) + bs.T[None, None, :, :, None]
    return u * mixed.reshape(bsz, s, MIX_W)


def conformer_conv_mixer(d, w, b, ln_g, ln_b):
    a, gate = jnp.split(d, 2, axis=-1)
    y = causal_depthwise_conv(a * jax.nn.sigmoid(gate), w, b)
    return jax.nn.silu(layernorm(y, ln_g, ln_b))


def memory_cross_attention(h, mem_n, wq, wk, wv, wo):
    bsz, s, _ = h.shape
    q = (h @ wq).reshape(bsz, s, XA_HEADS, XA_HEAD_DIM)
    k = (mem_n @ wk).reshape(bsz, -1, XA_HEADS, XA_HEAD_DIM)
    v = (mem_n @ wv).reshape(bsz, -1, XA_HEADS, XA_HEAD_DIM)
    scores = jnp.einsum('bshd,bmhd->bhsm', q.astype(jnp.float32), k.astype(jnp.float32))
    probs = jax.nn.softmax(scores * (XA_HEAD_DIM ** -0.5), axis=-1).astype(v.dtype)
    o = jnp.einsum('bhsm,bmhd->bshd', probs, v).reshape(bsz, s, D_MODEL)
    return o @ wo


def setup_inputs(seed: int = 0) -> dict:
    key = jax.random.key(seed)
    ks = iter(jax.random.split(key, 64))
    L, D = DEPTH, D_MODEL

    def w(shape, fan_in):
        return jax.random.normal(next(ks), shape, jnp.float32) * (fan_in ** -0.5)

    def gain(shape):
        return 1.0 + 0.05 * jax.random.normal(next(ks), shape, jnp.float32)

    def bias(shape, s=0.02):
        return s * jax.random.normal(next(ks), shape, jnp.float32)

    return {
        'x': jax.random.normal(next(ks), (BATCH, SEQ, D), jnp.float32),
        'mem': jax.random.normal(next(ks), (BATCH, N_MEM, D), jnp.float32),
        'ffn1_pre_g': gain((L, D)),
        'ffn1_post_g': gain((L, D)),
        'ffn1_w1': w((L, D, D_FF), D),
        'ffn1_w3': w((L, D, D_FF), D),
        'ffn1_w2': w((L, D_FF, D), D_FF),
        'mix_pre_g': gain((L, D)),
        'mix_post_g': gain((L, D)),
        'w_in': w((L, D, IN_COLS), D),
        'conv_a_w': w((L, SHORT_K, MIX_W), SHORT_K),
        'conv_a_b': bias((L, MIX_W)),
        'pool_w': w((L, GROUPS, GROUP_W, GROUP_W), GROUP_W),
        'pool_scale': gain((L, MIX_W)),
        'sgu_ln_g': gain((L, MIX_W)),
        'sgu_ln_b': bias((L, MIX_W)),
        'sgu_ws': w((L, GROUPS, SGU_BLOCK, SGU_BLOCK), SGU_BLOCK),
        'sgu_b': 1.0 + bias((L, GROUPS, SGU_BLOCK), 0.05),
        'conv_d_w': w((L, CONF_K, MIX_W), CONF_K),
        'conv_d_b': bias((L, MIX_W)),
        'conv_d_ln_g': gain((L, MIX_W)),
        'conv_d_ln_b': bias((L, MIX_W)),
        'w_branch': w((L, N_BRANCH, MIX_W, D), MIX_W),
        'w_gate': w((L, D, N_BRANCH * D), D),
        'b_gate': bias((L, N_BRANCH * D)),
        'w_o': w((L, D, D), D),
        'xa_pre_g': gain((L, D)),
        'xa_post_g': gain((L, D)),
        'mem_g': gain((L, D)),
        'xa_wq': w((L, D, D), D),
        'xa_wk': w((L, D, D), D),
        'xa_wv': w((L, D, D), D),
        'xa_wo': w((L, D, D), D),
        'ffn2_pre_g': gain((L, D)),
        'ffn2_post_g': gain((L, D)),
        'ffn2_w1': w((L, D, D_FF), D),
        'ffn2_w3': w((L, D, D_FF), D),
        'ffn2_w2': w((L, D_FF, D), D_FF),
    }


def reference(x, mem, ffn1_pre_g, ffn1_post_g, ffn1_w1, ffn1_w3, ffn1_w2,
              mix_pre_g, mix_post_g, w_in, conv_a_w, conv_a_b, pool_w, pool_scale,
              sgu_ln_g, sgu_ln_b, sgu_ws, sgu_b, conv_d_w, conv_d_b, conv_d_ln_g, conv_d_ln_b,
              w_branch, w_gate, b_gate, w_o,
              xa_pre_g, xa_post_g, mem_g, xa_wq, xa_wk, xa_wv, xa_wo,
              ffn2_pre_g, ffn2_post_g, ffn2_w1, ffn2_w3, ffn2_w2):
    bsz, s, _ = x.shape
    for l in range(DEPTH):
        h = rmsnorm(x, ffn1_pre_g[l])
        x = x + 0.5 * rmsnorm(swiglu(h, ffn1_w1[l], ffn1_w3[l], ffn1_w2[l]), ffn1_post_g[l])

        h = rmsnorm(x, mix_pre_g[l])
        z = h @ w_in[l]
        xa, ba, ca, p, gc, d = jnp.split(z, IN_SPLITS, axis=-1)
        y_a = short_conv_mixer(xa, ba, ca, conv_a_w[l], conv_a_b[l]) @ w_branch[l, 0]
        y_b = pool_mixer(p, pool_w[l], pool_scale[l]) @ w_branch[l, 1]
        y_c = sgu_mixer(gc, sgu_ln_g[l], sgu_ln_b[l], sgu_ws[l], sgu_b[l]) @ w_branch[l, 2]
        y_d = conformer_conv_mixer(d, conv_d_w[l], conv_d_b[l], conv_d_ln_g[l],
                                   conv_d_ln_b[l]) @ w_branch[l, 3]
        gates = jax.nn.sigmoid(h @ w_gate[l] + b_gate[l]).reshape(bsz, s, N_BRANCH, D_MODEL)
        merged = (gates[:, :, 0] * y_a + gates[:, :, 1] * y_b
                  + gates[:, :, 2] * y_c + gates[:, :, 3] * y_d)
        x = x + rmsnorm(merged @ w_o[l], mix_post_g[l])

        h = rmsnorm(x, xa_pre_g[l])
        mem_n = rmsnorm(mem, mem_g[l])
        xa_out = memory_cross_attention(h, mem_n, xa_wq[l], xa_wk[l], xa_wv[l], xa_wo[l])
        x = x + rmsnorm(xa_out, xa_post_g[l])

        h = rmsnorm(x, ffn2_pre_g[l])
        x = x + 0.5 * rmsnorm(swiglu(h, ffn2_w1[l], ffn2_w3[l], ffn2_w2[l]), ffn2_post_g[l])
    return x
```

```python
import functools

import jax
import jax.numpy as jnp
from jax import lax
from jax.experimental import pallas as pl
from jax.experimental.pallas import tpu as pltpu

F32 = jnp.float32
BF16 = jnp.bfloat16

EPS = 1e-6
CHUNK = 64
CHUNK_SHIFT = CHUNK.bit_length() - 1
assert 1 << CHUNK_SHIFT == CHUNK
GROUPS = 4
GROUP_W = 128
MIX_W = GROUPS * GROUP_W
SHORT_K = 3
POOL_WINDOWS = (2, 4, 8, 16)
SGU_BLOCK = 128
CONF_K = 31
XA_HEADS = 4

SUBLANES = 8
HALO_A = 8
HALO_B = 24
HALO_D = 32

MIB = 1024 * 1024


def _rms(x, g):
    return x * lax.rsqrt(jnp.mean(x * x, axis=-1, keepdims=True) + EPS) * g


def _layernorm(x, g, b):
    mu = jnp.mean(x, axis=-1, keepdims=True)
    xc = x - mu
    var = jnp.mean(xc * xc, axis=-1, keepdims=True)
    return xc * lax.rsqrt(var + EPS) * g + b


def _dot(a, b):
    return jnp.dot(a, b, preferred_element_type=F32)


def _const_spec(block_shape, index):
    return pl.BlockSpec(block_shape, lambda *_: index, pipeline_mode=pl.Buffered(1))


def _params(vmem_mib, n_axes):
    return pltpu.CompilerParams(dimension_semantics=("arbitrary",) * n_axes,
                                vmem_limit_bytes=vmem_mib * MIB)


def _ffn_kernel(x_ref, pre_ref, post_ref, w1_ref, w3_ref, w2_ref, o_ref, *, ff_chunk):
    x = x_ref[...]
    h = _rms(x, pre_ref[...]).astype(BF16)
    d_ff = w1_ref.shape[1]
    acc = jnp.zeros(x.shape, F32)
    for c in range(d_ff // ff_chunk):
        sl = slice(c * ff_chunk, (c + 1) * ff_chunk)
        a = _dot(h, w1_ref[:, sl])
        b = _dot(h, w3_ref[:, sl])
        hid = (jax.nn.silu(a) * b).astype(BF16)
        acc = acc + _dot(hid, w2_ref[sl, :])
    o_ref[...] = x + 0.5 * _rms(acc, post_ref[...])


def _ffn(x2, layer, pre_g, post_g, w1, w3, w2, *, tm=512, ff_chunk=256):
    t, d = x2.shape
    d_ff = w1.shape[2]
    return pl.pallas_call(
        functools.partial(_ffn_kernel, ff_chunk=ff_chunk),
        out_shape=jax.ShapeDtypeStruct((t, d), F32),
        grid=(t // tm,),
        in_specs=[
            pl.BlockSpec((tm, d), lambda i: (i, 0)),
            _const_spec((None, 1, d), (layer, 0, 0)),
            _const_spec((None, 1, d), (layer, 0, 0)),
            _const_spec((None, d, d_ff), (layer, 0, 0)),
            _const_spec((None, d, d_ff), (layer, 0, 0)),
            _const_spec((None, d_ff, d), (layer, 0, 0)),
        ],
        out_specs=pl.BlockSpec((tm, d), lambda i: (i, 0)),
        compiler_params=_params(44, 1),
        name="ffn",
    )(x2, pre_g, post_g, w1, w3, w2)


def _branches_kernel(x_ref, pre_ref, win_ref, caw_ref, cab_ref, pw_ref, ps_ref,
                     lng_ref, lnb_ref, ws_ref, sb_ref, cdw_ref, cdb_ref, dlg_ref, dlb_ref,
                     m_ref,
                     h_ref, ua_buf, p_buf, s2_buf, s4_buf, s8_buf, d_buf, y_ref, *, tm):
    j = pl.program_id(1)

    @pl.when(j == 0)
    def _():
        ua_buf[0:HALO_A, :] = jnp.zeros((HALO_A, MIX_W), F32)
        p_buf[0:HALO_B, :] = jnp.zeros((HALO_B, MIX_W), F32)
        d_buf[0, 0:HALO_D, :] = jnp.zeros((HALO_D, MIX_W), F32)
        s2_buf[0:SUBLANES, :] = jnp.zeros((SUBLANES, MIX_W), F32)
        s4_buf[0:SUBLANES, :] = jnp.zeros((SUBLANES, MIX_W), F32)
        s8_buf[0:SUBLANES, :] = jnp.zeros((SUBLANES, MIX_W), F32)

    h_ref[...] = _rms(x_ref[...], pre_ref[...]).astype(BF16)
    h = h_ref[...]

    zd = _dot(h, win_ref[:, 6 * MIX_W:8 * MIX_W])
    d_buf[0, HALO_D:HALO_D + tm, :] = zd[:, :MIX_W] * jax.nn.sigmoid(zd[:, MIX_W:])
    rows = HALO_D + tm
    for r in range(1, SUBLANES):
        d_buf[r, SUBLANES:rows, :] = d_buf[0, SUBLANES - r:rows - r, :]
    for r in range(SUBLANES):
        part = None
        for q in range((CONF_K - 1 - r) // SUBLANES + 1):
            lag = SUBLANES * q + r
            k = CONF_K - 1 - lag
            lo = HALO_D - SUBLANES * q
            term = cdw_ref[k:k + 1, :] * d_buf[r, lo:lo + tm, :]
            part = term if part is None else part + term
        if r == 0:
            y_ref[...] = part + cdb_ref[...]
        else:
            y_ref[...] += part
    yd = _layernorm(y_ref[...], dlg_ref[...], dlb_ref[...])
    m_ref[:, 3 * MIX_W:4 * MIX_W] = jax.nn.silu(yd).astype(BF16)
    d_buf[0, 0:HALO_D, :] = d_buf[0, tm:tm + HALO_D, :]

    za = _dot(h, win_ref[:, 0:3 * MIX_W])
    ua_buf[HALO_A:HALO_A + tm, :] = za[:, 2 * MIX_W:3 * MIX_W] * za[:, 0:MIX_W]
    ya = cab_ref[...]
    for k in range(SHORT_K):
        lo = HALO_A - (SHORT_K - 1) + k
        ya = ya + caw_ref[k:k + 1, :] * ua_buf[lo:lo + tm, :]
    m_ref[:, 0:MIX_W] = (za[:, MIX_W:2 * MIX_W] * ya).astype(BF16)
    ua_buf[0:HALO_A, :] = ua_buf[tm:tm + HALO_A, :]

    p_buf[HALO_B:HALO_B + tm, :] = _dot(h, win_ref[:, 3 * MIX_W:4 * MIX_W])
    rows = HALO_B + tm
    s2_buf[SUBLANES:rows, :] = p_buf[SUBLANES:rows, :] + p_buf[SUBLANES - 1:rows - 1, :]
    s4_buf[SUBLANES:rows, :] = s2_buf[SUBLANES:rows, :] + s2_buf[SUBLANES - 2:rows - 2, :]
    s8_buf[SUBLANES:rows, :] = s4_buf[SUBLANES:rows, :] + s4_buf[SUBLANES - 4:rows - 4, :]
    pos1 = (j * tm + 1 + lax.broadcasted_iota(jnp.int32, (tm, 1), 0)).astype(F32)
    p_cur = p_buf[HALO_B:rows, :]
    sums = (s2_buf, s4_buf, s8_buf)
    for g, win in enumerate(POOL_WINDOWS):
        gl = slice(g * GROUP_W, (g + 1) * GROUP_W)
        if win == 16:
            ssum = s8_buf[HALO_B:rows, gl] + s8_buf[HALO_B - 8:rows - 8, gl]
        else:
            ssum = sums[g][HALO_B:rows, gl]
        pooled = ssum / jnp.minimum(pos1, float(win)) - p_cur[:, gl]
        yb = _dot(pooled.astype(BF16), pw_ref[g]) * ps_ref[:, gl]
        m_ref[:, MIX_W + g * GROUP_W:MIX_W + (g + 1) * GROUP_W] = yb.astype(BF16)
    p_buf[0:HALO_B, :] = p_buf[tm:tm + HALO_B, :]

    gc = jax.nn.gelu(_dot(h, win_ref[:, 4 * MIX_W:6 * MIX_W]))
    u = gc[:, :MIX_W]
    vn = _layernorm(gc[:, MIX_W:], lng_ref[...], lnb_ref[...]).astype(BF16)
    ci = lax.broadcasted_iota(jnp.int32, (SGU_BLOCK, SGU_BLOCK), 0) >> CHUNK_SHIFT
    cj = lax.broadcasted_iota(jnp.int32, (SGU_BLOCK, SGU_BLOCK), 1) >> CHUNK_SHIFT
    for g in range(GROUPS):
        gl = slice(g * GROUP_W, (g + 1) * GROUP_W)
        wm = jnp.where(cj <= ci, ws_ref[g], 0.0).astype(BF16)
        for n in range(tm // SGU_BLOCK):
            rs = slice(n * SGU_BLOCK, (n + 1) * SGU_BLOCK)
            mixed = _dot(wm, vn[rs, gl]) + sb_ref[:, gl]
            m_ref[rs, 2 * MIX_W + g * GROUP_W:2 * MIX_W + (g + 1) * GROUP_W] = (u[rs, gl] * mixed).astype(BF16)


def _branches(x, layer, pre_g, w_in, caw, cab, pool_w, pool_s, lng, lnb, ws, sb, cdw, cdb, dlg, dlb, *, tm=256):
    bsz, s, d = x.shape
    c = MIX_W
    row = lambda n: _const_spec((None, 1, n), (layer, 0, 0))
    return pl.pallas_call(
        functools.partial(_branches_kernel, tm=tm),
        out_shape=jax.ShapeDtypeStruct((bsz, s, 4 * c), BF16),
        grid=(bsz, s // tm),
        in_specs=[
            pl.BlockSpec((None, tm, d), lambda b, j: (b, j, 0)),
            row(d),
            _const_spec((None, d, 8 * c), (layer, 0, 0)),
            _const_spec((None, SHORT_K, c), (layer, 0, 0)),
            row(c),
            _const_spec((None, GROUPS, GROUP_W, GROUP_W), (layer, 0, 0, 0)),
            row(c),
            row(c),
            row(c),
            _const_spec((None, GROUPS, SGU_BLOCK, SGU_BLOCK), (layer, 0, 0, 0)),
            _const_spec((None, SGU_BLOCK, c), (layer, 0, 0)),
            _const_spec((None, CONF_K, c), (layer, 0, 0)),
            row(c),
            row(c),
            row(c),
        ],
        out_specs=pl.BlockSpec((None, tm, 4 * c), lambda b, j: (b, j, 0)),
        scratch_shapes=[
            pltpu.VMEM((tm, d), BF16),
            pltpu.VMEM((HALO_A + tm, c), F32),
            pltpu.VMEM((HALO_B + tm, c), F32),
            pltpu.VMEM((HALO_B + tm, c), F32),
            pltpu.VMEM((HALO_B + tm, c), F32),
            pltpu.VMEM((HALO_B + tm, c), F32),
            pltpu.VMEM((SUBLANES, HALO_D + tm, c), F32),
            pltpu.VMEM((tm, c), F32),
        ],
        compiler_params=_params(44, 2),
        name="branches",
    )(x, pre_g, w_in, caw, cab, pool_w, pool_s, lng, lnb, ws, sb, cdw, cdb, dlg, dlb)


def _merge_kernel(x_ref, m_ref, pre_ref, post_ref, wg_ref, bg_ref, wb_ref, wo_ref, o_ref):
    x = x_ref[...]
    d = x.shape[1]
    h = _rms(x, pre_ref[...]).astype(BF16)
    merged = None
    for i in range(wb_ref.shape[0]):
        cols = slice(i * d, (i + 1) * d)
        gate = jax.nn.sigmoid(_dot(h, wg_ref[:, cols]) + bg_ref[:, cols])
        term = gate * _dot(m_ref[:, i * MIX_W:(i + 1) * MIX_W], wb_ref[i])
        merged = term if merged is None else merged + term
    out = _dot(merged.astype(BF16), wo_ref[...])
    o_ref[...] = x + _rms(out, post_ref[...])


def _merge(x2, m2, layer, pre_g, post_g, w_gate, b_gate, w_branch, w_o, *, tm=512):
    t, d = x2.shape
    nb = w_branch.shape[1]
    return pl.pallas_call(
        _merge_kernel,
        out_shape=jax.ShapeDtypeStruct((t, d), F32),
        grid=(t // tm,),
        in_specs=[
            pl.BlockSpec((tm, d), lambda i: (i, 0)),
            pl.BlockSpec((tm, nb * MIX_W), lambda i: (i, 0)),
            _const_spec((None, 1, d), (layer, 0, 0)),
            _const_spec((None, 1, d), (layer, 0, 0)),
            _const_spec((None, d, nb * d), (layer, 0, 0)),
            _const_spec((None, 1, nb * d), (layer, 0, 0)),
            _const_spec((None, nb, MIX_W, d), (layer, 0, 0, 0)),
            _const_spec((None, d, d), (layer, 0, 0)),
        ],
        out_specs=pl.BlockSpec((tm, d), lambda i: (i, 0)),
        compiler_params=_params(44, 1),
        name="merge",
    )(x2, m2, pre_g, post_g, w_gate, b_gate, w_branch, w_o)


def _kv_kernel(mem_ref, g_ref, wk_ref, wv_ref, k_ref, v_ref):
    mn = _rms(mem_ref[...], g_ref[...]).astype(BF16)
    k_ref[...] = _dot(mn, wk_ref[...]).astype(BF16)
    v_ref[...] = _dot(mn, wv_ref[...]).astype(BF16)


def _kv(mem, mem_g, wk, wv):
    bsz, n_mem, d = mem.shape
    n_layers = wk.shape[0]
    out = jax.ShapeDtypeStruct((n_layers, bsz, n_mem, d), BF16)
    return pl.pallas_call(
        _kv_kernel,
        out_shape=(out, out),
        grid=(n_layers, bsz),
        in_specs=[
            pl.BlockSpec((None, n_mem, d), lambda l, b: (b, 0, 0)),
            pl.BlockSpec((None, 1, d), lambda l, b: (l, 0, 0)),
            pl.BlockSpec((None, d, d), lambda l, b: (l, 0, 0)),
            pl.BlockSpec((None, d, d), lambda l, b: (l, 0, 0)),
        ],
        out_specs=(pl.BlockSpec((None, None, n_mem, d), lambda l, b: (l, b, 0, 0)),
                   pl.BlockSpec((None, None, n_mem, d), lambda l, b: (l, b, 0, 0))),
        compiler_params=_params(32, 2),
        name="memory_kv",
    )(mem, mem_g, wk, wv)


def _xattn_kernel(x_ref, k_ref, v_ref, pre_ref, post_ref, wq_ref, wo_ref, o_ref):
    x = x_ref[...]
    d = x.shape[1]
    hd = d // XA_HEADS
    h = _rms(x, pre_ref[...]).astype(BF16)
    q = _dot(h, wq_ref[...])
    heads = []
    for i in range(XA_HEADS):
        cols = slice(i * hd, (i + 1) * hd)
        s = lax.dot_general(q[:, cols].astype(BF16), k_ref[:, cols], (((1,), (1,)), ((), ())),
                            preferred_element_type=F32) * (hd ** -0.5)
        e = jnp.exp(s - jnp.max(s, axis=-1, keepdims=True))
        p = e / jnp.sum(e, axis=-1, keepdims=True)
        heads.append(_dot(p.astype(BF16), v_ref[:, cols]).astype(BF16))
    out = _dot(jnp.concatenate(heads, axis=-1), wo_ref[...])
    o_ref[...] = x + _rms(out, post_ref[...])


def _xattn(x, k, v, layer, pre_g, post_g, wq, wo, *, tm=512):
    bsz, s, d = x.shape
    n_mem = k.shape[2]
    return pl.pallas_call(
        _xattn_kernel,
        out_shape=jax.ShapeDtypeStruct((bsz, s, d), F32),
        grid=(bsz, s // tm),
        in_specs=[
            pl.BlockSpec((None, tm, d), lambda b, j: (b, j, 0)),
            pl.BlockSpec((None, None, n_mem, d), lambda b, j: (layer, b, 0, 0)),
            pl.BlockSpec((None, None, n_mem, d), lambda b, j: (layer, b, 0, 0)),
            _const_spec((None, 1, d), (layer, 0, 0)),
            _const_spec((None, 1, d), (layer, 0, 0)),
            _const_spec((None, d, d), (layer, 0, 0)),
            _const_spec((None, d, d), (layer, 0, 0)),
        ],
        out_specs=pl.BlockSpec((None, tm, d), lambda b, j: (b, j, 0)),
        compiler_params=_params(32, 2),
        name="xattn",
    )(x, k, v, pre_g, post_g, wq, wo)


def kernel(x, mem, ffn1_pre_g, ffn1_post_g, ffn1_w1, ffn1_w3, ffn1_w2, mix_pre_g, mix_post_g, w_in, conv_a_w, conv_a_b, pool_w, pool_scale, sgu_ln_g, sgu_ln_b, sgu_ws, sgu_b, conv_d_w, conv_d_b, conv_d_ln_g, conv_d_ln_b, w_branch, w_gate, b_gate, w_o, xa_pre_g, xa_post_g, mem_g, xa_wq, xa_wk, xa_wv, xa_wo, ffn2_pre_g, ffn2_post_g, ffn2_w1, ffn2_w3, ffn2_w2):
    bsz, s, d = x.shape
    n_layers = w_in.shape[0]
    bf = lambda w: w.astype(BF16)
    row = lambda v: v[:, None, :]

    ffn1 = (row(ffn1_pre_g), row(ffn1_post_g), bf(ffn1_w1), bf(ffn1_w3), bf(ffn1_w2))
    ffn2 = (row(ffn2_pre_g), row(ffn2_post_g), bf(ffn2_w1), bf(ffn2_w3), bf(ffn2_w2))
    sgu_bias = jnp.repeat(jnp.swapaxes(sgu_b, 1, 2), GROUP_W, axis=2)
    branch_args = (row(mix_pre_g), bf(w_in), conv_a_w, row(conv_a_b), bf(pool_w), row(pool_scale),
                   row(sgu_ln_g), row(sgu_ln_b), sgu_ws, sgu_bias, conv_d_w, row(conv_d_b),
                   row(conv_d_ln_g), row(conv_d_ln_b))
    merge_args = (row(mix_pre_g), row(mix_post_g), bf(w_gate), row(b_gate), bf(w_branch), bf(w_o))
    xa_args = (row(xa_pre_g), row(xa_post_g), bf(xa_wq), bf(xa_wo))
    k_all, v_all = _kv(mem, row(mem_g), bf(xa_wk), bf(xa_wv))

    for layer in range(n_layers):
        x2 = _ffn(x.reshape(bsz * s, d), layer, *ffn1)
        m = _branches(x2.reshape(bsz, s, d), layer, *branch_args)
        x2 = _merge(x2, m.reshape(bsz * s, -1), layer, *merge_args)
        x = _xattn(x2.reshape(bsz, s, d), k_all, v_all, layer, *xa_args)
        x = _ffn(x.reshape(bsz * s, d), layer, *ffn2).reshape(bsz, s, d)
    return x
```
